```python
import jax, jax.numpy as jnp
from jax import lax
import numpy as np

D_MODEL = 1024
BATCH = 4
SEQ = 4096
DEPTH = 1

CHUNK = 64
LEFT_CHUNKS = 8
BAND = (LEFT_CHUNKS + 1) * CHUNK
D_MIX = D_MODEL
D_ATTN = D_MIX // 2
HEAD_DIM = 64
N_HEADS = D_ATTN // HEAD_DIM
D_CONV = D_MIX - D_ATTN
CONV_GROUPS = 8
CONV_WIDTH = 31
MAX_REL = 128
D_FF = 2816
D_IN = 3 * D_ATTN + 2 * D_CONV
EPS = 1e-6
NEG_INF = -1e30

kernel_name = "hymba_conformer_chunk_attn_conv_block"


def rmsnorm(x, g):
    x32 = x.astype(jnp.float32)
    r = x32 * lax.rsqrt(jnp.mean(x32 * x32, axis=-1, keepdims=True) + EPS)
    return (r * g.astype(jnp.float32)).astype(x.dtype)


def layernorm(x, g, b):
    x32 = x.astype(jnp.float32)
    mu = jnp.mean(x32, axis=-1, keepdims=True)
    var = jnp.mean(jnp.square(x32 - mu), axis=-1, keepdims=True)
    r = (x32 - mu) * lax.rsqrt(var + EPS)
    return (r * g.astype(jnp.float32) + b.astype(jnp.float32)).astype(x.dtype)


def swiglu(h, w_gate, w_up, w_down):
    return (jax.nn.silu(h @ w_gate) * (h @ w_up)) @ w_down


def key_band(t, n_chunks):
    b, _, h, d = t.shape
    tc = t.reshape(b, n_chunks, CHUNK, h, d)
    tp = jnp.pad(tc, ((0, 0), (LEFT_CHUNKS, 0), (0, 0), (0, 0), (0, 0)))
    return jnp.concatenate([tp[:, j:j + n_chunks] for j in range(LEFT_CHUNKS + 1)], axis=2)


def chunk_band_attention(q, k, v, rel_bias):
    b, t, h, d = q.shape
    n_chunks = t // CHUNK
    qc = q.reshape(b, n_chunks, CHUNK, h, d)
    kb = key_band(k, n_chunks)
    vb = key_band(v, n_chunks)
    qi = jnp.arange(CHUNK)[:, None]
    kj = jnp.arange(BAND)[None, :]
    idx = jnp.clip(qi + LEFT_CHUNKS * CHUNK - kj, -MAX_REL, MAX_REL) + MAX_REL
    bias = jnp.transpose(rel_bias[idx], (2, 0, 1)).astype(jnp.float32)
    key_pos = (jnp.arange(n_chunks)[:, None] - LEFT_CHUNKS) * CHUNK + kj
    valid = key_pos >= 0
    scale = 1.0 / np.sqrt(HEAD_DIM)
    s = jnp.einsum('bcqhd,bckhd->bchqk', qc, kb).astype(jnp.float32) * scale
    s = s + bias[None, None]
    s = jnp.where(valid[None, :, None, None, :], s, NEG_INF)
    p = jax.nn.softmax(s, axis=-1).astype(v.dtype)
    o = jnp.einsum('bchqk,bckhd->bcqhd', p, vb)
    return o.reshape(b, t, h * d)


def conformer_conv(u, dw_kernel, dw_bias, ln_g, ln_b):
    g = u[..., :D_CONV] * jax.nn.sigmoid(u[..., D_CONV:])
    dw = lax.conv_general_dilated(
        g, dw_kernel[:, None, :].astype(g.dtype), window_strides=(1,),
        padding=[(CONV_WIDTH - 1, 0)],
        dimension_numbers=('NWC', 'WIO', 'NWC'),
        feature_group_count=D_CONV) + dw_bias
    return jax.nn.silu(layernorm(dw, ln_g, ln_b))


def setup_inputs(seed: int = 0) -> dict:
    key = jax.random.key(seed)
    ks = jax.random.split(key, 20)
    L = DEPTH
    f32 = jnp.float32

    def normal(k, shape, scale):
        return jax.random.normal(k, shape, f32) * scale

    def gain(k, shape):
        return 1.0 + 0.02 * jax.random.normal(k, shape, f32)

    return {
        "x": jax.random.normal(ks[0], (BATCH, SEQ, D_MODEL), f32),
        "ffn1_norm": gain(ks[1], (L, D_MODEL)),
        "ffn1_gate": normal(ks[2], (L, D_MODEL, D_FF), D_MODEL ** -0.5),
        "ffn1_up": normal(ks[3], (L, D_MODEL, D_FF), D_MODEL ** -0.5),
        "ffn1_down": normal(ks[4], (L, D_FF, D_MODEL), D_FF ** -0.5),
        "mix_norm": gain(ks[5], (L, D_MODEL)),
        "w_in": normal(ks[6], (L, D_MODEL, D_IN), D_MODEL ** -0.5),
        "rel_bias": normal(ks[7], (L, 2 * MAX_REL + 1, N_HEADS), 0.1),
        "dw_kernel": normal(ks[8], (L, CONV_WIDTH, D_CONV), CONV_WIDTH ** -0.5),
        "dw_bias": normal(ks[9], (L, D_CONV), 0.02),
        "conv_ln_g": gain(ks[10], (L, D_CONV)),
        "conv_ln_b": normal(ks[11], (L, D_CONV), 0.02),
        "w_out": normal(ks[12], (L, D_MIX, D_MODEL), D_MIX ** -0.5),
        "ffn2_norm": gain(ks[13], (L, D_MODEL)),
        "ffn2_gate": normal(ks[14], (L, D_MODEL, D_FF), D_MODEL ** -0.5),
        "ffn2_up": normal(ks[15], (L, D_MODEL, D_FF), D_MODEL ** -0.5),
        "ffn2_down": normal(ks[16], (L, D_FF, D_MODEL), D_FF ** -0.5),
        "final_norm": gain(ks[17], (D_MODEL,)),
    }


def reference(x, ffn1_norm, ffn1_gate, ffn1_up, ffn1_down, mix_norm, w_in, rel_bias,
              dw_kernel, dw_bias, conv_ln_g, conv_ln_b, w_out, ffn2_norm, ffn2_gate,
              ffn2_up, ffn2_down, final_norm):
    b, t, _ = x.shape
    for l in range(DEPTH):
        x = x + 0.5 * swiglu(rmsnorm(x, ffn1_norm[l]), ffn1_gate[l], ffn1_up[l], ffn1_down[l])
        z = rmsnorm(x, mix_norm[l]) @ w_in[l]
        q = z[..., 0 * D_ATTN:1 * D_ATTN].reshape(b, t, N_HEADS, HEAD_DIM)
        k = z[..., 1 * D_ATTN:2 * D_ATTN].reshape(b, t, N_HEADS, HEAD_DIM)
        v = z[..., 2 * D_ATTN:3 * D_ATTN].reshape(b, t, N_HEADS, HEAD_DIM)
        u = z[..., 3 * D_ATTN:]
        attn_out = chunk_band_attention(q, k, v, rel_bias[l])
        conv_out = conformer_conv(u, dw_kernel[l], dw_bias[l], conv_ln_g[l], conv_ln_b[l])
        x = x + jnp.concatenate([attn_out, conv_out], axis=-1) @ w_out[l]
        x = x + 0.5 * swiglu(rmsnorm(x, ffn2_norm[l]), ffn2_gate[l], ffn2_up[l], ffn2_down[l])
    return rmsnorm(x, final_norm)
```

```python
import functools

import jax
import jax.numpy as jnp
import numpy as np
from jax import lax
from jax.experimental import pallas as pl
from jax.experimental.pallas import tpu as pltpu

D_MODEL = 1024
CHUNK = 64
LEFT_CHUNKS = 8
D_ATTN = 512
HEAD_DIM = 64
N_HEADS = 8
D_CONV = 512
CONV_WIDTH = 31
MAX_REL = 128
D_FF = 2816
EPS = 1e-6
NEG_INF = -1e30

V7X_LANES = 128
V7X_MXU_DIM = 256
V7X_VMEM_BYTES = 64 * 1024 * 1024

FFN_ROWS = 512
FF_CHUNK = V7X_MXU_DIM
ATT_ROWS = 4 * CHUNK
ATT_KEYS = ATT_ROWS + LEFT_CHUNKS * CHUNK
ATT_KBLKS = ATT_KEYS // ATT_ROWS
CONV_ROWS = 512
CONV_HALO = 32
CONV_SUB = 64
FFN_VMEM_LIMIT = 56 * 1024 * 1024


def _rmsnorm(x32, g):
    return x32 * lax.rsqrt(jnp.mean(x32 * x32, axis=-1, keepdims=True) + EPS) * g


def _dot(a, b):
    return jnp.dot(a, b, preferred_element_type=jnp.float32)


def _swiglu_half_step(x32, norm_g, wg_ref, wu_ref, wd_ref, act_ref):
    h = _rmsnorm(x32, norm_g).astype(jnp.bfloat16)
    for c in range(D_FF // FF_CHUNK):
        cols = slice(c * FF_CHUNK, (c + 1) * FF_CHUNK)
        g = _dot(h, wg_ref[:, cols])
        u = _dot(h, wu_ref[:, cols])
        act_ref[:, cols] = (g * jax.nn.sigmoid(g) * u).astype(jnp.bfloat16)
    return x32 + 0.5 * _dot(act_ref[...], wd_ref[...])


def _ffn1_inproj_kernel(x_ref, n1_ref, wg_ref, wu_ref, wd_ref, n2_ref,
                        wq_ref, wkt_ref, wv_ref, wc_ref,
                        x1_ref, q_ref, kt_ref, v_ref, u_ref, act_ref):
    x1 = _swiglu_half_step(x_ref[...], n1_ref[...], wg_ref, wu_ref, wd_ref, act_ref)
    x1_ref[...] = x1
    h = _rmsnorm(x1, n2_ref[...]).astype(jnp.bfloat16)
    q_ref[...] = _dot(h, wq_ref[...]).astype(jnp.bfloat16)
    kt = lax.dot_general(wkt_ref[...], h, (((1,), (1,)), ((), ())),
                         preferred_element_type=jnp.float32)
    kt_ref[0] = kt.astype(jnp.bfloat16)
    v_ref[...] = _dot(h, wv_ref[...]).astype(jnp.bfloat16)
    u_ref[...] = _dot(h, wc_ref[...])


def _band_attn_kernel(q_ref, kt0_ref, kt1_ref, kt2_ref, v0_ref, v1_ref, v2_ref,
                      bias_ref, o_ref):
    j = pl.program_id(1)
    kt_refs = (kt0_ref, kt1_ref, kt2_ref)
    v_refs = (v0_ref, v1_ref, v2_ref)
    blk_neg = [jnp.where(j >= ATT_KBLKS - 1 - i, 0.0, NEG_INF).astype(jnp.float32)
               for i in range(ATT_KBLKS - 1)] + [None]
    lane = lax.broadcasted_iota(jnp.int32, (ATT_ROWS, V7X_LANES), 1)
    low_half = lane < HEAD_DIM
    for pair in range(N_HEADS // 2):
        lanes = slice(pair * V7X_LANES, (pair + 1) * V7X_LANES)
        q_pair = q_ref[:, lanes]
        outs = []
        for e in range(2):
            head = 2 * pair + e
            qm = jnp.where(low_half if e == 0 else ~low_half, q_pair, jnp.zeros_like(q_pair))
            s = []
            for i in range(ATT_KBLKS):
                si = _dot(qm, kt_refs[i][0, lanes, :])
                si = si + bias_ref[head, :, i * ATT_ROWS:(i + 1) * ATT_ROWS]
                if blk_neg[i] is not None:
                    si = si + blk_neg[i]
                s.append(si)
            m = jnp.maximum(jnp.maximum(s[0], s[1]), s[2]).max(axis=-1, keepdims=True)
            p = [jnp.exp(si - m) for si in s]
            l = (p[0] + p[1] + p[2]).sum(axis=-1, keepdims=True)
            o = _dot(p[0].astype(jnp.bfloat16), v_refs[0][:, lanes])
            o = o + _dot(p[1].astype(jnp.bfloat16), v_refs[1][:, lanes])
            o = o + _dot(p[2].astype(jnp.bfloat16), v_refs[2][:, lanes])
            outs.append(o / l)
        o_ref[:, lanes] = jnp.where(low_half, outs[0], outs[1]).astype(o_ref.dtype)


def _glu_conv_kernel(u_ref, halo_ref, k_ref, b_ref, lg_ref, lb_ref, o_ref, g_ref):
    j = pl.program_id(1)
    ua = u_ref[:, :D_CONV]
    ub = u_ref[:, D_CONV:]
    g_ref[CONV_HALO:, :] = ua * jax.nn.sigmoid(ub)
    ha = halo_ref[:, :D_CONV]
    hb = halo_ref[:, D_CONV:]
    hg = ha * jax.nn.sigmoid(hb)
    g_ref[:CONV_HALO, :] = jnp.where(j > 0, hg, 0.0)

    for r0 in range(0, CONV_ROWS, CONV_SUB):
        acc = jnp.zeros((CONV_SUB, D_CONV), jnp.float32)
        for t in range(CONV_WIDTH):
            off = r0 + CONV_HALO - (CONV_WIDTH - 1) + t
            acc = acc + g_ref[off:off + CONV_SUB, :] * k_ref[t:t + 1, :]
        dw = acc + b_ref[...]
        mu = jnp.mean(dw, axis=-1, keepdims=True)
        var = jnp.mean(jnp.square(dw - mu), axis=-1, keepdims=True)
        y = (dw - mu) * lax.rsqrt(var + EPS) * lg_ref[...] + lb_ref[...]
        o_ref[r0:r0 + CONV_SUB, :] = (y * jax.nn.sigmoid(y)).astype(o_ref.dtype)


def _outproj_ffn2_kernel(x1_ref, a_ref, c_ref, woa_ref, woc_ref, n3_ref,
                         wg_ref, wu_ref, wd_ref, nf_ref, y_ref, act_ref):
    x2 = x1_ref[...] + _dot(a_ref[...], woa_ref[...]) + _dot(c_ref[...], woc_ref[...])
    x3 = _swiglu_half_step(x2, n3_ref[...], wg_ref, wu_ref, wd_ref, act_ref)
    y_ref[...] = _rmsnorm(x3, nf_ref[...])


def _resident(shape):
    return pl.BlockSpec(shape, lambda *_: (0,) * len(shape), pipeline_mode=pl.Buffered(1))


def _rel_bias_table(rel_bias):
    qi = np.arange(ATT_ROWS)[:, None]
    kj = np.arange(ATT_KEYS)[None, :]
    idx = np.clip(qi + LEFT_CHUNKS * CHUNK - kj, -MAX_REL, MAX_REL) + MAX_REL
    dc = kj // CHUNK - qi // CHUNK
    in_band = (dc >= 0) & (dc <= LEFT_CHUNKS)
    bias = jnp.transpose(rel_bias[idx], (2, 0, 1)).astype(jnp.float32)
    return jnp.where(in_band[None], bias, NEG_INF)


def kernel(x, ffn1_norm, ffn1_gate, ffn1_up, ffn1_down, mix_norm, w_in, rel_bias, dw_kernel, dw_bias, conv_ln_g, conv_ln_b, w_out, ffn2_norm, ffn2_gate, ffn2_up, ffn2_down, final_norm):
    b, t, d = x.shape
    n = b * t
    assert d == D_MODEL and t % FFN_ROWS == 0 and t % ATT_ROWS == 0 and t % CONV_ROWS == 0
    bf = jnp.bfloat16
    x2d = x.reshape(n, d)
    row = lambda v: v.reshape(1, -1).astype(jnp.float32)

    w_in0 = w_in[0]
    wq = (w_in0[:, :D_ATTN] * (1.0 / np.sqrt(HEAD_DIM))).astype(bf)
    wkt = w_in0[:, D_ATTN:2 * D_ATTN].T.astype(bf)
    wv = w_in0[:, 2 * D_ATTN:3 * D_ATTN].astype(bf)
    wc = w_in0[:, 3 * D_ATTN:].astype(bf)

    tiles_per_seq = t // FFN_ROWS
    row_tile = lambda cols: pl.BlockSpec((FFN_ROWS, cols), lambda i: (i, 0))
    ffn_weights = [_resident((D_MODEL, D_FF)), _resident((D_MODEL, D_FF)),
                   _resident((D_FF, D_MODEL))]
    ffn_params = pltpu.CompilerParams(dimension_semantics=("parallel",),
                                      vmem_limit_bytes=FFN_VMEM_LIMIT)

    x1, q, kt, v, u = pl.pallas_call(
        _ffn1_inproj_kernel,
        grid=(n // FFN_ROWS,),
        in_specs=[row_tile(D_MODEL), _resident((1, D_MODEL))] + ffn_weights + [
            _resident((1, D_MODEL)), _resident((D_MODEL, D_ATTN)),
            _resident((D_ATTN, D_MODEL)), _resident((D_MODEL, D_ATTN)),
            _resident((D_MODEL, 2 * D_CONV))],
        out_specs=[row_tile(D_MODEL), row_tile(D_ATTN),
                   pl.BlockSpec((1, D_ATTN, FFN_ROWS),
                                lambda i: (i // tiles_per_seq, 0, i % tiles_per_seq)),
                   row_tile(D_ATTN), row_tile(2 * D_CONV)],
        out_shape=[jax.ShapeDtypeStruct((n, D_MODEL), jnp.float32),
                   jax.ShapeDtypeStruct((n, D_ATTN), bf),
                   jax.ShapeDtypeStruct((b, D_ATTN, t), bf),
                   jax.ShapeDtypeStruct((n, D_ATTN), bf),
                   jax.ShapeDtypeStruct((n, 2 * D_CONV), jnp.float32)],
        scratch_shapes=[pltpu.VMEM((FFN_ROWS, D_FF), bf)],
        compiler_params=ffn_params,
        name="ffn1_inproj",
    )(x2d, row(ffn1_norm[0]), ffn1_gate[0].astype(bf), ffn1_up[0].astype(bf),
      ffn1_down[0].astype(bf), row(mix_norm[0]), wq, wkt, wv, wc)

    att_blocks = t // ATT_ROWS
    kt_spec = lambda i: pl.BlockSpec(
        (1, D_ATTN, ATT_ROWS),
        lambda bi, j: (bi, 0, jnp.maximum(j - (ATT_KBLKS - 1 - i), 0)))
    v_spec = lambda i: pl.BlockSpec(
        (ATT_ROWS, D_ATTN),
        lambda bi, j: (bi * att_blocks + jnp.maximum(j - (ATT_KBLKS - 1 - i), 0), 0))
    attn = pl.pallas_call(
        _band_attn_kernel,
        grid=(b, att_blocks),
        in_specs=[pl.BlockSpec((ATT_ROWS, D_ATTN), lambda bi, j: (bi * att_blocks + j, 0))]
        + [kt_spec(i) for i in range(ATT_KBLKS)] + [v_spec(i) for i in range(ATT_KBLKS)]
        + [_resident((N_HEADS, ATT_ROWS, ATT_KEYS))],
        out_specs=pl.BlockSpec((ATT_ROWS, D_ATTN), lambda bi, j: (bi * att_blocks + j, 0)),
        out_shape=jax.ShapeDtypeStruct((n, D_ATTN), bf),
        compiler_params=pltpu.CompilerParams(dimension_semantics=("parallel", "parallel")),
        name="band_attn",
    )(q, kt, kt, kt, v, v, v, _rel_bias_table(rel_bias[0]))

    conv_blocks = t // CONV_ROWS
    halo_per_tile = CONV_ROWS // CONV_HALO
    conv = pl.pallas_call(
        _glu_conv_kernel,
        grid=(b, conv_blocks),
        in_specs=[
            pl.BlockSpec((CONV_ROWS, 2 * D_CONV), lambda bi, j: (bi * conv_blocks + j, 0)),
            pl.BlockSpec((CONV_HALO, 2 * D_CONV),
                         lambda bi, j: (jnp.maximum((bi * conv_blocks + j) * halo_per_tile - 1, 0), 0)),
            _resident((CONV_WIDTH, D_CONV)), _resident((1, D_CONV)),
            _resident((1, D_CONV)), _resident((1, D_CONV))],
        out_specs=pl.BlockSpec((CONV_ROWS, D_CONV), lambda bi, j: (bi * conv_blocks + j, 0)),
        out_shape=jax.ShapeDtypeStruct((n, D_CONV), bf),
        scratch_shapes=[pltpu.VMEM((CONV_HALO + CONV_ROWS, D_CONV), jnp.float32)],
        compiler_params=pltpu.CompilerParams(dimension_semantics=("parallel", "parallel")),
        name="glu_conv",
    )(u, u, dw_kernel[0].astype(jnp.float32), row(dw_bias[0]), row(conv_ln_g[0]),
      row(conv_ln_b[0]))

    w_out0 = w_out[0]
    y = pl.pallas_call(
        _outproj_ffn2_kernel,
        grid=(n // FFN_ROWS,),
        in_specs=[row_tile(D_MODEL), row_tile(D_ATTN), row_tile(D_CONV),
                  _resident((D_ATTN, D_MODEL)), _resident((D_CONV, D_MODEL)),
                  _resident((1, D_MODEL))] + ffn_weights + [_resident((1, D_MODEL))],
        out_specs=row_tile(D_MODEL),
        out_shape=jax.ShapeDtypeStruct((n, D_MODEL), jnp.float32),
        scratch_shapes=[pltpu.VMEM((FFN_ROWS, D_FF), bf)],
        compiler_params=ffn_params,
        name="outproj_ffn2",
    )(x1, attn, conv, w_out0[:D_ATTN].astype(bf), w_out0[D_ATTN:].astype(bf),
      row(ffn2_norm[0]), ffn2_gate[0].astype(bf), ffn2_up[0].astype(bf),
      ffn2_down[0].astype(bf), row(final_norm))
    return y.reshape(b, t, d)
```

```python
import functools

import jax
import jax.numpy as jnp
import numpy as np
from jax import lax
from jax.experimental import pallas as pl
from jax.experimental.pallas import tpu as pltpu

D_MODEL = 1024
CHUNK = 64
LEFT_CHUNKS = 8
D_ATTN = 512
HEAD_DIM = 64
N_HEADS = 8
D_CONV = 512
CONV_WIDTH = 31
MAX_REL = 128
D_FF = 2816
EPS = 1e-6
NEG_INF = -1e30

V7X_LANES = 128
V7X_SUBLANES = 8
V7X_MXU_DIM = 256
V7X_VMEM_BYTES = 64 * 1024 * 1024

FFN_ROWS = 512
FF_CHUNK = V7X_MXU_DIM
ATT_ROWS = 4 * CHUNK
ATT_KEYS = ATT_ROWS + LEFT_CHUNKS * CHUNK
ATT_KBLKS = ATT_KEYS // ATT_ROWS
BIAS_SPAN = 1024
CONV_ROWS = 512
CONV_HALO = 32
CONV_SUB = 64
FFN_VMEM_LIMIT = 56 * 1024 * 1024


def _rmsnorm(x32, g):
    return x32 * lax.rsqrt(jnp.mean(x32 * x32, axis=-1, keepdims=True) + EPS) * g


def _dot(a, b):
    return jnp.dot(a, b, preferred_element_type=jnp.float32)


def _swiglu_half_step(x32, norm_g, wg_ref, wu_ref, wd_ref, act_ref):
    h = _rmsnorm(x32, norm_g).astype(jnp.bfloat16)
    for c in range(D_FF // FF_CHUNK):
        cols = slice(c * FF_CHUNK, (c + 1) * FF_CHUNK)
        g = _dot(h, wg_ref[:, cols])
        u = _dot(h, wu_ref[:, cols])
        act_ref[:, cols] = (g * jax.nn.sigmoid(g) * u).astype(jnp.bfloat16)
    return x32 + 0.5 * _dot(act_ref[...], wd_ref[...])


def _ffn1_inproj_kernel(x_ref, n1_ref, wg_ref, wu_ref, wd_ref, n2_ref,
                        wq_ref, wkt_ref, wv_ref, wc_ref,
                        x1_ref, q_ref, kt_ref, v_ref, u_ref, act_ref):
    x1 = _swiglu_half_step(x_ref[...], n1_ref[...], wg_ref, wu_ref, wd_ref, act_ref)
    x1_ref[...] = x1
    h = _rmsnorm(x1, n2_ref[...]).astype(jnp.bfloat16)
    q_ref[...] = _dot(h, wq_ref[...]).astype(jnp.bfloat16)
    kt = lax.dot_general(wkt_ref[...], h, (((1,), (1,)), ((), ())),
                         preferred_element_type=jnp.float32)
    kt_ref[0] = kt.astype(jnp.bfloat16)
    v_ref[...] = _dot(h, wv_ref[...]).astype(jnp.bfloat16)
    u_ref[...] = _dot(h, wc_ref[...])


def _band_attn_kernel(q_ref, kt0_ref, kt1_ref, kt2_ref, v0_ref, v1_ref, v2_ref,
                      bias_ref, o_ref):
    j = pl.program_id(1)
    kt_refs = (kt0_ref, kt1_ref, kt2_ref)
    v_refs = (v0_ref, v1_ref, v2_ref)
    blk_neg = [jnp.where(j >= ATT_KBLKS - 1 - i, 0.0, NEG_INF).astype(jnp.float32)
               for i in range(ATT_KBLKS - 1)] + [None]
    lane = lax.broadcasted_iota(jnp.int32, (ATT_ROWS, V7X_LANES), 1)
    low_half = lane < HEAD_DIM
    for pair in range(N_HEADS // 2):
        lanes = slice(pair * V7X_LANES, (pair + 1) * V7X_LANES)
        q_pair = q_ref[:, lanes]
        outs = []
        for e in range(2):
            head = 2 * pair + e
            qm = jnp.where(low_half if e == 0 else ~low_half, q_pair, jnp.zeros_like(q_pair))
            s = []
            for i in range(ATT_KBLKS):
                si = _dot(qm, kt_refs[i][0, lanes, :])
                si = si + bias_ref[head, :, i * ATT_ROWS:(i + 1) * ATT_ROWS]
                if blk_neg[i] is not None:
                    si = si + blk_neg[i]
                s.append(si)
            m = jnp.maximum(jnp.maximum(s[0], s[1]), s[2]).max(axis=-1, keepdims=True)
            p = [jnp.exp(si - m) for si in s]
            l = (p[0] + p[1] + p[2]).sum(axis=-1, keepdims=True)
            o = _dot(p[0].astype(jnp.bfloat16), v_refs[0][:, lanes])
            o = o + _dot(p[1].astype(jnp.bfloat16), v_refs[1][:, lanes])
            o = o + _dot(p[2].astype(jnp.bfloat16), v_refs[2][:, lanes])
            outs.append(o / l)
        o_ref[:, lanes] = jnp.where(low_half, outs[0], outs[1]).astype(o_ref.dtype)


def _glu_conv_kernel(u_ref, halo_ref, k_ref, b_ref, lg_ref, lb_ref, o_ref, g_ref, dw_ref):
    j = pl.program_id(1)
    ua = u_ref[:, :D_CONV]
    ub = u_ref[:, D_CONV:]
    g_ref[CONV_HALO:, :] = ua * jax.nn.sigmoid(ub)
    ha = halo_ref[:, :D_CONV]
    hb = halo_ref[:, D_CONV:]
    hg = ha * jax.nn.sigmoid(hb)
    g_ref[:CONV_HALO, :] = jnp.where(j > 0, hg, 0.0)

    first = CONV_HALO - (CONV_WIDTH - 1)
    for c0 in range(0, D_CONV, V7X_LANES):
        lanes = slice(c0, c0 + V7X_LANES)
        for r0 in range(0, CONV_ROWS, CONV_SUB):
            acc = None
            for r in range(V7X_SUBLANES):
                rows = CONV_SUB + (V7X_SUBLANES if r else 0)
                part = None
                for tap in range(CONV_WIDTH):
                    if (first + tap) % V7X_SUBLANES != r:
                        continue
                    a0 = r0 + (first + tap) - r
                    term = g_ref[a0:a0 + rows, lanes] * k_ref[tap:tap + 1, lanes]
                    part = term if part is None else part + term
                part = part[r:r + CONV_SUB]
                acc = part if acc is None else acc + part
            dw_ref[r0:r0 + CONV_SUB, lanes] = acc + b_ref[:, lanes]

    for r0 in range(0, CONV_ROWS, CONV_SUB):
        dw = dw_ref[r0:r0 + CONV_SUB, :]
        mu = jnp.mean(dw, axis=-1, keepdims=True)
        var = jnp.mean(jnp.square(dw - mu), axis=-1, keepdims=True)
        y = (dw - mu) * lax.rsqrt(var + EPS) * lg_ref[...] + lb_ref[...]
        o_ref[r0:r0 + CONV_SUB, :] = (y * jax.nn.sigmoid(y)).astype(o_ref.dtype)


def _outproj_ffn2_kernel(x1_ref, a_ref, c_ref, woa_ref, woc_ref, n3_ref,
                         wg_ref, wu_ref, wd_ref, nf_ref, y_ref, act_ref):
    x2 = x1_ref[...] + _dot(a_ref[...], woa_ref[...]) + _dot(c_ref[...], woc_ref[...])
    x3 = _swiglu_half_step(x2, n3_ref[...], wg_ref, wu_ref, wd_ref, act_ref)
    y_ref[...] = _rmsnorm(x3, nf_ref[...])


def _resident(shape):
    return pl.BlockSpec(shape, lambda *_: (0,) * len(shape), pipeline_mode=pl.Buffered(1))


def _rel_bias_kernel(t_ref, o_ref):
    head = pl.program_id(0)
    m = lax.broadcasted_iota(jnp.int32, (1, BIAS_SPAN), 1)
    offset = jnp.where(m < ATT_KEYS, m, m - BIAS_SPAN)
    idx = jnp.clip(LEFT_CHUNKS * CHUNK - offset, -MAX_REL, MAX_REL) + MAX_REL

    def pick(i, row):
        return jnp.where(idx == i, t_ref[i * N_HEADS + head], row)

    row = lax.fori_loop(0, 2 * MAX_REL + 1, pick, jnp.zeros((1, BIAS_SPAN), jnp.float32))
    x = jnp.broadcast_to(row, (V7X_SUBLANES, BIAS_SPAN))
    sub = lax.broadcasted_iota(jnp.int32, (V7X_SUBLANES, BIAS_SPAN), 0)
    for bit in (1, 2, 4):
        x = jnp.where((sub & bit) != 0, pltpu.roll(x, bit, axis=1), x)
    key_chunk = lax.broadcasted_iota(jnp.int32, (V7X_SUBLANES, ATT_KEYS), 1) // CHUNK
    for g in range(ATT_ROWS // V7X_SUBLANES):
        q0 = g * V7X_SUBLANES
        rows = x if g == 0 else pltpu.roll(x, q0, axis=1)
        dc = key_chunk - q0 // CHUNK
        in_band = (dc >= 0) & (dc <= LEFT_CHUNKS)
        o_ref[0, q0:q0 + V7X_SUBLANES, :] = jnp.where(in_band, rows[:, :ATT_KEYS], NEG_INF)


def _rel_bias_table(rel_bias):
    return pl.pallas_call(
        _rel_bias_kernel,
        grid=(N_HEADS,),
        in_specs=[pl.BlockSpec(memory_space=pltpu.SMEM)],
        out_specs=pl.BlockSpec((1, ATT_ROWS, ATT_KEYS), lambda h: (h, 0, 0)),
        out_shape=jax.ShapeDtypeStruct((N_HEADS, ATT_ROWS, ATT_KEYS), jnp.float32),
        compiler_params=pltpu.CompilerParams(dimension_semantics=("parallel",)),
        name="rel_bias_table",
    )(rel_bias.reshape(-1).astype(jnp.float32))


def kernel(x, ffn1_norm, ffn1_gate, ffn1_up, ffn1_down, mix_norm, w_in, rel_bias, dw_kernel, dw_bias, conv_ln_g, conv_ln_b, w_out, ffn2_norm, ffn2_gate, ffn2_up, ffn2_down, final_norm):
    b, t, d = x.shape
    n = b * t
    assert d == D_MODEL and t % FFN_ROWS == 0 and t % ATT_ROWS == 0 and t % CONV_ROWS == 0
    bf = jnp.bfloat16
    x2d = x.reshape(n, d)
    row = lambda v: v.reshape(1, -1).astype(jnp.float32)

    w_in0 = w_in[0]
    wq = (w_in0[:, :D_ATTN] * (1.0 / np.sqrt(HEAD_DIM))).astype(bf)
    wkt = w_in0[:, D_ATTN:2 * D_ATTN].T.astype(bf)
    wv = w_in0[:, 2 * D_ATTN:3 * D_ATTN].astype(bf)
    wc = w_in0[:, 3 * D_ATTN:].astype(bf)

    tiles_per_seq = t // FFN_ROWS
    row_tile = lambda cols: pl.BlockSpec((FFN_ROWS, cols), lambda i: (i, 0))
    ffn_weights = [_resident((D_MODEL, D_FF)), _resident((D_MODEL, D_FF)),
                   _resident((D_FF, D_MODEL))]
    ffn_params = pltpu.CompilerParams(dimension_semantics=("parallel",),
                                      vmem_limit_bytes=FFN_VMEM_LIMIT)

    x1, q, kt, v, u = pl.pallas_call(
        _ffn1_inproj_kernel,
        grid=(n // FFN_ROWS,),
        in_specs=[row_tile(D_MODEL), _resident((1, D_MODEL))] + ffn_weights + [
            _resident((1, D_MODEL)), _resident((D_MODEL, D_ATTN)),
            _resident((D_ATTN, D_MODEL)), _resident((D_MODEL, D_ATTN)),
            _resident((D_MODEL, 2 * D_CONV))],
        out_specs=[row_tile(D_MODEL), row_tile(D_ATTN),
                   pl.BlockSpec((1, D_ATTN, FFN_ROWS),
                                lambda i: (i // tiles_per_seq, 0, i % tiles_per_seq)),
                   row_tile(D_ATTN), row_tile(2 * D_CONV)],
        out_shape=[jax.ShapeDtypeStruct((n, D_MODEL), jnp.float32),
                   jax.ShapeDtypeStruct((n, D_ATTN), bf),
                   jax.ShapeDtypeStruct((b, D_ATTN, t), bf),
                   jax.ShapeDtypeStruct((n, D_ATTN), bf),
                   jax.ShapeDtypeStruct((n, 2 * D_CONV), jnp.float32)],
        scratch_shapes=[pltpu.VMEM((FFN_ROWS, D_FF), bf)],
        compiler_params=ffn_params,
        name="ffn1_inproj",
    )(x2d, row(ffn1_norm[0]), ffn1_gate[0].astype(bf), ffn1_up[0].astype(bf),
      ffn1_down[0].astype(bf), row(mix_norm[0]), wq, wkt, wv, wc)

    att_blocks = t // ATT_ROWS
    kt_spec = lambda i: pl.BlockSpec(
        (1, D_ATTN, ATT_ROWS),
        lambda bi, j: (bi, 0, jnp.maximum(j - (ATT_KBLKS - 1 - i), 0)))
    v_spec = lambda i: pl.BlockSpec(
        (ATT_ROWS, D_ATTN),
        lambda bi, j: (bi * att_blocks + jnp.maximum(j - (ATT_KBLKS - 1 - i), 0), 0))
    attn = pl.pallas_call(
        _band_attn_kernel,
        grid=(b, att_blocks),
        in_specs=[pl.BlockSpec((ATT_ROWS, D_ATTN), lambda bi, j: (bi * att_blocks + j, 0))]
        + [kt_spec(i) for i in range(ATT_KBLKS)] + [v_spec(i) for i in range(ATT_KBLKS)]
        + [_resident((N_HEADS, ATT_ROWS, ATT_KEYS))],
        out_specs=pl.BlockSpec((ATT_ROWS, D_ATTN), lambda bi, j: (bi * att_blocks + j, 0)),
        out_shape=jax.ShapeDtypeStruct((n, D_ATTN), bf),
        compiler_params=pltpu.CompilerParams(dimension_semantics=("parallel", "parallel")),
        name="band_attn",
    )(q, kt, kt, kt, v, v, v, _rel_bias_table(rel_bias[0]))

    conv_blocks = t // CONV_ROWS
    halo_per_tile = CONV_ROWS // CONV_HALO
    conv = pl.pallas_call(
        _glu_conv_kernel,
        grid=(b, conv_blocks),
        in_specs=[
            pl.BlockSpec((CONV_ROWS, 2 * D_CONV), lambda bi, j: (bi * conv_blocks + j, 0)),
            pl.BlockSpec((CONV_HALO, 2 * D_CONV),
                         lambda bi, j: (jnp.maximum((bi * conv_blocks + j) * halo_per_tile - 1, 0), 0)),
            _resident((CONV_WIDTH, D_CONV)), _resident((1, D_CONV)),
            _resident((1, D_CONV)), _resident((1, D_CONV))],
        out_specs=pl.BlockSpec((CONV_ROWS, D_CONV), lambda bi, j: (bi * conv_blocks + j, 0)),
        out_shape=jax.ShapeDtypeStruct((n, D_CONV), bf),
        scratch_shapes=[pltpu.VMEM((CONV_HALO + CONV_ROWS, D_CONV), jnp.float32),
                        pltpu.VMEM((CONV_ROWS, D_CONV), jnp.float32)],
        compiler_params=pltpu.CompilerParams(dimension_semantics=("parallel", "parallel")),
        name="glu_conv",
    )(u, u, dw_kernel[0].astype(jnp.float32), row(dw_bias[0]), row(conv_ln_g[0]),
      row(conv_ln_b[0]))

    w_out0 = w_out[0]
    y = pl.pallas_call(
        _outproj_ffn2_kernel,
        grid=(n // FFN_ROWS,),
        in_specs=[row_tile(D_MODEL), row_tile(D_ATTN), row_tile(D_CONV),
                  _resident((D_ATTN, D_MODEL)), _resident((D_CONV, D_MODEL)),
                  _resident((1, D_MODEL))] + ffn_weights + [_resident((1, D_MODEL))],
        out_specs=row_tile(D_MODEL),
        out_shape=jax.ShapeDtypeStruct((n, D_MODEL), jnp.float32),
        scratch_shapes=[pltpu.VMEM((FFN_ROWS, D_FF), bf)],
        compiler_params=ffn_params,
        name="outproj_ffn2",
    )(x1, attn, conv, w_out0[:D_ATTN].astype(bf), w_out0[D_ATTN:].astype(bf),
      row(ffn2_norm[0]), ffn2_gate[0].astype(bf), ffn2_up[0].astype(bf),
      ffn2_down[0].astype(bf), row(final_norm))
    return y.reshape(b, t, d)
```

```python
import functools

import jax
import jax.numpy as jnp
import numpy as np
from jax import lax
from jax.experimental import pallas as pl
from jax.experimental.pallas import tpu as pltpu

D_MODEL = 1024
CHUNK = 64
LEFT_CHUNKS = 8
D_ATTN = 512
HEAD_DIM = 64
N_HEADS = 8
D_CONV = 512
CONV_WIDTH = 31
MAX_REL = 128
D_FF = 2816
EPS = 1e-6
NEG_INF = -1e30
LOG2E = float(np.log2(np.e))

V7X_LANES = 128
V7X_SUBLANES = 8
V7X_MXU_DIM = 256
V7X_VMEM_BYTES = 64 * 1024 * 1024

FFN_ROWS = 512
FF_CHUNK = V7X_MXU_DIM
ATT_ROWS = 4 * CHUNK
ATT_KEYS = ATT_ROWS + LEFT_CHUNKS * CHUNK
ATT_KBLKS = ATT_KEYS // ATT_ROWS
ATT_LOOKAHEAD = 2
BIAS_SPAN = 1024
CONV_ROWS = 512
CONV_HALO = 32
CONV_SUB = 64
FFN_VMEM_LIMIT = 56 * 1024 * 1024


def _rmsnorm(x32, g):
    return x32 * lax.rsqrt(jnp.mean(x32 * x32, axis=-1, keepdims=True) + EPS) * g


def _dot(a, b):
    return jnp.dot(a, b, preferred_element_type=jnp.float32)


def _swiglu_half_step(x32, norm_g, wg_ref, wu_ref, wd_ref, act_ref):
    h = _rmsnorm(x32, norm_g).astype(jnp.bfloat16)
    for c in range(D_FF // FF_CHUNK):
        cols = slice(c * FF_CHUNK, (c + 1) * FF_CHUNK)
        g = _dot(h, wg_ref[:, cols])
        u = _dot(h, wu_ref[:, cols])
        act_ref[:, cols] = (g * jax.nn.sigmoid(g) * u).astype(jnp.bfloat16)
    return x32 + 0.5 * _dot(act_ref[...], wd_ref[...])


def _ffn1_inproj_kernel(x_ref, n1_ref, wg_ref, wu_ref, wd_ref, n2_ref,
                        wq_ref, wkt_ref, wv_ref, wc_ref,
                        x1_ref, q_ref, kt_ref, v_ref, u_ref, act_ref):
    x1 = _swiglu_half_step(x_ref[...], n1_ref[...], wg_ref, wu_ref, wd_ref, act_ref)
    x1_ref[...] = x1
    h = _rmsnorm(x1, n2_ref[...]).astype(jnp.bfloat16)
    q_ref[...] = (_dot(h, wq_ref[...]) * LOG2E).astype(jnp.bfloat16)
    kt = lax.dot_general(wkt_ref[...], h, (((1,), (1,)), ((), ())),
                         preferred_element_type=jnp.float32)
    kt_ref[0] = kt.astype(jnp.bfloat16)
    v_ref[...] = _dot(h, wv_ref[...]).astype(jnp.bfloat16)
    u_ref[...] = _dot(h, wc_ref[...])


def _band_attn_kernel(q_ref, kt0_ref, kt1_ref, kt2_ref, v0_ref, v1_ref, v2_ref,
                      bias_ref, o_ref):
    j = pl.program_id(1)
    kt_refs = (kt0_ref, kt1_ref, kt2_ref)
    v_refs = (v0_ref, v1_ref, v2_ref)

    def attend(key_mask_row):
        lane = lax.broadcasted_iota(jnp.int32, (ATT_ROWS, V7X_LANES), 1)
        low_half = lane < HEAD_DIM
        pair_lanes = lambda head: slice((head // 2) * V7X_LANES, (head // 2 + 1) * V7X_LANES)

        def scores(head):
            lanes = pair_lanes(head)
            q_pair = q_ref[:, lanes]
            kt = jnp.concatenate([r[0, lanes, :] for r in kt_refs], axis=1)
            mine = low_half if head % 2 == 0 else ~low_half
            s = _dot(jnp.where(mine, q_pair, jnp.zeros_like(q_pair)), kt) + bias_ref[head]
            return s if key_mask_row is None else s + key_mask_row

        ahead = [scores(h) for h in range(ATT_LOOKAHEAD)]
        outs = []
        for head in range(N_HEADS):
            s = ahead.pop(0)
            if head + ATT_LOOKAHEAD < N_HEADS:
                ahead.append(scores(head + ATT_LOOKAHEAD))
            p = jnp.exp2(s - s.max(axis=-1, keepdims=True))
            l = p.sum(axis=-1, keepdims=True)
            lanes = pair_lanes(head)
            vv = jnp.concatenate([r[:, lanes] for r in v_refs], axis=0)
            outs.append(_dot(p.astype(jnp.bfloat16), vv) / l)
            if head % 2 == 1:
                o_ref[:, lanes] = jnp.where(low_half, outs[-2], outs[-1]).astype(o_ref.dtype)

    @pl.when(j >= ATT_KBLKS - 1)
    def _():
        attend(None)

    @pl.when(j < ATT_KBLKS - 1)
    def _():
        col_blk = lax.broadcasted_iota(jnp.int32, (1, ATT_KEYS), 1) // ATT_ROWS
        attend(jnp.where(col_blk + j >= ATT_KBLKS - 1, 0.0, NEG_INF).astype(jnp.float32))


def _glu_conv_kernel(u_ref, halo_ref, k_ref, b_ref, lg_ref, lb_ref, o_ref, g_ref, dw_ref):
    j = pl.program_id(1)
    ua = u_ref[:, :D_CONV]
    ub = u_ref[:, D_CONV:]
    g_ref[CONV_HALO:, :] = ua * jax.nn.sigmoid(ub)
    ha = halo_ref[:, :D_CONV]
    hb = halo_ref[:, D_CONV:]
    hg = ha * jax.nn.sigmoid(hb)
    g_ref[:CONV_HALO, :] = jnp.where(j > 0, hg, 0.0)

    first = CONV_HALO - (CONV_WIDTH - 1)
    for c0 in range(0, D_CONV, V7X_LANES):
        lanes = slice(c0, c0 + V7X_LANES)
        for r0 in range(0, CONV_ROWS, CONV_SUB):
            acc = None
            for r in range(V7X_SUBLANES):
                rows = CONV_SUB + (V7X_SUBLANES if r else 0)
                part = None
                for tap in range(CONV_WIDTH):
                    if (first + tap) % V7X_SUBLANES != r:
                        continue
                    a0 = r0 + (first + tap) - r
                    term = g_ref[a0:a0 + rows, lanes] * k_ref[tap:tap + 1, lanes]
                    part = term if part is None else part + term
                part = part[r:r + CONV_SUB]
                acc = part if acc is None else acc + part
            dw_ref[r0:r0 + CONV_SUB, lanes] = acc + b_ref[:, lanes]

    for r0 in range(0, CONV_ROWS, CONV_SUB):
        dw = dw_ref[r0:r0 + CONV_SUB, :]
        mu = jnp.mean(dw, axis=-1, keepdims=True)
        var = jnp.mean(jnp.square(dw - mu), axis=-1, keepdims=True)
        y = (dw - mu) * lax.rsqrt(var + EPS) * lg_ref[...] + lb_ref[...]
        o_ref[r0:r0 + CONV_SUB, :] = (y * jax.nn.sigmoid(y)).astype(o_ref.dtype)


def _outproj_ffn2_kernel(x1_ref, a_ref, c_ref, woa_ref, woc_ref, n3_ref,
                         wg_ref, wu_ref, wd_ref, nf_ref, y_ref, act_ref):
    x2 = x1_ref[...] + _dot(a_ref[...], woa_ref[...]) + _dot(c_ref[...], woc_ref[...])
    x3 = _swiglu_half_step(x2, n3_ref[...], wg_ref, wu_ref, wd_ref, act_ref)
    y_ref[...] = _rmsnorm(x3, nf_ref[...])


def _resident(shape):
    return pl.BlockSpec(shape, lambda *_: (0,) * len(shape), pipeline_mode=pl.Buffered(1))


def _rel_bias_kernel(t_ref, o_ref):
    head = pl.program_id(0)
    m = lax.broadcasted_iota(jnp.int32, (1, BIAS_SPAN), 1)
    offset = jnp.where(m < ATT_KEYS, m, m - BIAS_SPAN)
    idx = jnp.clip(LEFT_CHUNKS * CHUNK - offset, -MAX_REL, MAX_REL) + MAX_REL

    def pick(i, row):
        return jnp.where(idx == i, t_ref[i * N_HEADS + head], row)

    row = lax.fori_loop(0, 2 * MAX_REL + 1, pick, jnp.zeros((1, BIAS_SPAN), jnp.float32))
    x = jnp.broadcast_to(row, (V7X_SUBLANES, BIAS_SPAN))
    sub = lax.broadcasted_iota(jnp.int32, (V7X_SUBLANES, BIAS_SPAN), 0)
    for bit in (1, 2, 4):
        x = jnp.where((sub & bit) != 0, pltpu.roll(x, bit, axis=1), x)
    key_chunk = lax.broadcasted_iota(jnp.int32, (V7X_SUBLANES, ATT_KEYS), 1) // CHUNK
    for g in range(ATT_ROWS // V7X_SUBLANES):
        q0 = g * V7X_SUBLANES
        rows = x if g == 0 else pltpu.roll(x, q0, axis=1)
        dc = key_chunk - q0 // CHUNK
        in_band = (dc >= 0) & (dc <= LEFT_CHUNKS)
        o_ref[0, q0:q0 + V7X_SUBLANES, :] = jnp.where(in_band, rows[:, :ATT_KEYS] * LOG2E, NEG_INF)


def _rel_bias_table(rel_bias):
    return pl.pallas_call(
        _rel_bias_kernel,
        grid=(N_HEADS,),
        in_specs=[pl.BlockSpec(memory_space=pltpu.SMEM)],
        out_specs=pl.BlockSpec((1, ATT_ROWS, ATT_KEYS), lambda h: (h, 0, 0)),
        out_shape=jax.ShapeDtypeStruct((N_HEADS, ATT_ROWS, ATT_KEYS), jnp.float32),
        compiler_params=pltpu.CompilerParams(dimension_semantics=("parallel",)),
        name="rel_bias_table",
    )(rel_bias.reshape(-1).astype(jnp.float32))


def kernel(x, ffn1_norm, ffn1_gate, ffn1_up, ffn1_down, mix_norm, w_in, rel_bias, dw_kernel, dw_bias, conv_ln_g, conv_ln_b, w_out, ffn2_norm, ffn2_gate, ffn2_up, ffn2_down, final_norm):
    b, t, d = x.shape
    n = b * t
    assert d == D_MODEL and t % FFN_ROWS == 0 and t % ATT_ROWS == 0 and t % CONV_ROWS == 0
    bf = jnp.bfloat16
    x2d = x.reshape(n, d)
    row = lambda v: v.reshape(1, -1).astype(jnp.float32)

    w_in0 = w_in[0]
    wq = (w_in0[:, :D_ATTN] * (1.0 / np.sqrt(HEAD_DIM))).astype(bf)
    wkt = w_in0[:, D_ATTN:2 * D_ATTN].T.astype(bf)
    wv = w_in0[:, 2 * D_ATTN:3 * D_ATTN].astype(bf)
    wc = w_in0[:, 3 * D_ATTN:].astype(bf)

    tiles_per_seq = t // FFN_ROWS
    row_tile = lambda cols: pl.BlockSpec((FFN_ROWS, cols), lambda i: (i, 0))
    ffn_weights = [_resident((D_MODEL, D_FF)), _resident((D_MODEL, D_FF)),
                   _resident((D_FF, D_MODEL))]
    ffn_params = pltpu.CompilerParams(dimension_semantics=("parallel",),
                                      vmem_limit_bytes=FFN_VMEM_LIMIT)

    x1, q, kt, v, u = pl.pallas_call(
        _ffn1_inproj_kernel,
        grid=(n // FFN_ROWS,),
        in_specs=[row_tile(D_MODEL), _resident((1, D_MODEL))] + ffn_weights + [
            _resident((1, D_MODEL)), _resident((D_MODEL, D_ATTN)),
            _resident((D_ATTN, D_MODEL)), _resident((D_MODEL, D_ATTN)),
            _resident((D_MODEL, 2 * D_CONV))],
        out_specs=[row_tile(D_MODEL), row_tile(D_ATTN),
                   pl.BlockSpec((1, D_ATTN, FFN_ROWS),
                                lambda i: (i // tiles_per_seq, 0, i % tiles_per_seq)),
                   row_tile(D_ATTN), row_tile(2 * D_CONV)],
        out_shape=[jax.ShapeDtypeStruct((n, D_MODEL), jnp.float32),
                   jax.ShapeDtypeStruct((n, D_ATTN), bf),
                   jax.ShapeDtypeStruct((b, D_ATTN, t), bf),
                   jax.ShapeDtypeStruct((n, D_ATTN), bf),
                   jax.ShapeDtypeStruct((n, 2 * D_CONV), jnp.float32)],
        scratch_shapes=[pltpu.VMEM((FFN_ROWS, D_FF), bf)],
        compiler_params=ffn_params,
        name="ffn1_inproj",
    )(x2d, row(ffn1_norm[0]), ffn1_gate[0].astype(bf), ffn1_up[0].astype(bf),
      ffn1_down[0].astype(bf), row(mix_norm[0]), wq, wkt, wv, wc)

    att_blocks = t // ATT_ROWS
    kt_spec = lambda i: pl.BlockSpec(
        (1, D_ATTN, ATT_ROWS),
        lambda bi, j: (bi, 0, jnp.maximum(j - (ATT_KBLKS - 1 - i), 0)))
    v_spec = lambda i: pl.BlockSpec(
        (ATT_ROWS, D_ATTN),
        lambda bi, j: (bi * att_blocks + jnp.maximum(j - (ATT_KBLKS - 1 - i), 0), 0))
    attn = pl.pallas_call(
        _band_attn_kernel,
        grid=(b, att_blocks),
        in_specs=[pl.BlockSpec((ATT_ROWS, D_ATTN), lambda bi, j: (bi * att_blocks + j, 0))]
        + [kt_spec(i) for i in range(ATT_KBLKS)] + [v_spec(i) for i in range(ATT_KBLKS)]
        + [_resident((N_HEADS, ATT_ROWS, ATT_KEYS))],
        out_specs=pl.BlockSpec((ATT_ROWS, D_ATTN), lambda bi, j: (bi * att_blocks + j, 0)),
        out_shape=jax.ShapeDtypeStruct((n, D_ATTN), bf),
        compiler_params=pltpu.CompilerParams(dimension_semantics=("parallel", "parallel")),
        name="band_attn",
    )(q, kt, kt, kt, v, v, v, _rel_bias_table(rel_bias[0]))

    conv_blocks = t // CONV_ROWS
    halo_per_tile = CONV_ROWS // CONV_HALO
    conv = pl.pallas_call(
        _glu_conv_kernel,
        grid=(b, conv_blocks),
        in_specs=[
            pl.BlockSpec((CONV_ROWS, 2 * D_CONV), lambda bi, j: (bi * conv_blocks + j, 0)),
            pl.BlockSpec((CONV_HALO, 2 * D_CONV),
                         lambda bi, j: (jnp.maximum((bi * conv_blocks + j) * halo_per_tile - 1, 0), 0)),
            _resident((CONV_WIDTH, D_CONV)), _resident((1, D_CONV)),
            _resident((1, D_CONV)), _resident((1, D_CONV))],
        out_specs=pl.BlockSpec((CONV_ROWS, D_CONV), lambda bi, j: (bi * conv_blocks + j, 0)),
        out_shape=jax.ShapeDtypeStruct((n, D_CONV), bf),
        scratch_shapes=[pltpu.VMEM((CONV_HALO + CONV_ROWS, D_CONV), jnp.float32),
                        pltpu.VMEM((CONV_ROWS, D_CONV), jnp.float32)],
        compiler_params=pltpu.CompilerParams(dimension_semantics=("parallel", "parallel")),
        name="glu_conv",
    )(u, u, dw_kernel[0].astype(jnp.float32), row(dw_bias[0]), row(conv_ln_g[0]),
      row(conv_ln_b[0]))

    w_out0 = w_out[0]
    y = pl.pallas_call(
        _outproj_ffn2_kernel,
        grid=(n // FFN_ROWS,),
        in_specs=[row_tile(D_MODEL), row_tile(D_ATTN), row_tile(D_CONV),
                  _resident((D_ATTN, D_MODEL)), _resident((D_CONV, D_MODEL)),
                  _resident((1, D_MODEL))] + ffn_weights + [_resident((1, D_MODEL))],
        out_specs=row_tile(D_MODEL),
        out_shape=jax.ShapeDtypeStruct((n, D_MODEL), jnp.float32),
        scratch_shapes=[pltpu.VMEM((FFN_ROWS, D_FF), bf)],
        compiler_params=ffn_params,
        name="outproj_ffn2",
    )(x1, attn, conv, w_out0[:D_ATTN].astype(bf), w_out0[D_ATTN:].astype(bf),
      row(ffn2_norm[0]), ffn2_gate[0].astype(bf), ffn2_up[0].astype(bf),
      ffn2_down[0].astype(bf), row(final_norm))
    return y.reshape(b, t, d)
```
